```python
import math
import jax, jax.numpy as jnp
from jax import lax
import numpy as np

D_MODEL = 1024
BATCH = 8
SEQ = 8192
DEPTH = 2

N_META = 16
EPS = 1e-6
MASK_VALUE = -1e30

SWA_HEADS = 8
SWA_KV_HEADS = 2
SWA_HEAD_DIM = 64
SWA_WINDOW = 128
SWA_BLOCK = 128

MLA_HEADS = 8
MLA_Q_LORA = 384
MLA_KV_LORA = 256
MLA_NOPE_DIM = 64
MLA_ROPE_DIM = 32
MLA_V_DIM = 64
MLA_QUERY_BLOCK = 128
ROPE_THETA = 10000.0

ATTN_ALIGN = 128
ATTN_SPLITS = (SWA_HEADS * SWA_HEAD_DIM, SWA_KV_HEADS * SWA_HEAD_DIM, SWA_KV_HEADS * SWA_HEAD_DIM,
               MLA_Q_LORA, MLA_KV_LORA, MLA_ROPE_DIM)
ATTN_IN = sum(ATTN_SPLITS)
ATTN_MIX = SWA_HEADS * SWA_HEAD_DIM + MLA_HEADS * MLA_V_DIM

GLA_HEADS = 4
GLA_KEY_DIM = D_MODEL // 2
GLA_VALUE_DIM = D_MODEL
GLA_HEAD_K = GLA_KEY_DIM // GLA_HEADS
GLA_HEAD_V = GLA_VALUE_DIM // GLA_HEADS
GLA_GATE_RANK = 16
GLA_GATE_TAU = 16.0
GLA_CHUNK = 64
GLA_SPLITS = (GLA_KEY_DIM, GLA_KEY_DIM, GLA_VALUE_DIM, GLA_VALUE_DIM, GLA_GATE_RANK)
GLA_IN = sum(GLA_SPLITS)

PEER_N_KEYS = 128
PEER_EXPERTS = PEER_N_KEYS ** 2
PEER_HEADS = 8
PEER_TOPK = 16
PEER_KEY_DIM = 128
PEER_TOKEN_BLOCK = 256

N_ATTN_LAYERS = (DEPTH + 1) // 2
N_GLA_LAYERS = DEPTH // 2

kernel_name = "hybrid_swa_mla_gla_peer_meta"


def rmsnorm(x, g):
    xf = x.astype(jnp.float32)
    y = xf * lax.rsqrt(jnp.mean(xf * xf, axis=-1, keepdims=True) + EPS)
    return (y * g.astype(jnp.float32)).astype(x.dtype)


def split_cols(t, sizes):
    idx = [int(v) for v in np.cumsum(sizes)[:-1]]
    return jnp.split(t, idx, axis=-1)


def rope_tables(n, offset):
    pos = (jnp.arange(n) - offset).astype(jnp.float32)
    inv = ROPE_THETA ** (-jnp.arange(0, MLA_ROPE_DIM, 2, dtype=jnp.float32) / MLA_ROPE_DIM)
    ang = pos[:, None] * inv[None, :]
    return jnp.cos(ang), jnp.sin(ang)


def apply_rope(x, cos, sin):
    xf = x.astype(jnp.float32)
    x1, x2 = jnp.split(xf, 2, axis=-1)
    return jnp.concatenate([x1 * cos - x2 * sin, x2 * cos + x1 * sin], axis=-1).astype(x.dtype)


def swa_sink_attention(q, k, v, sinks, valid):
    B, Lp, Hq, d = q.shape
    nb = Lp // SWA_BLOCK
    G = Hq // SWA_KV_HEADS
    qb = q.reshape(B, nb, SWA_BLOCK, SWA_KV_HEADS, G, d)

    def with_prev(t):
        tb = t.reshape(t.shape[0], nb, SWA_BLOCK, *t.shape[2:])
        prev = jnp.concatenate([jnp.zeros_like(tb[:, :1]), tb[:, :-1]], axis=1)
        return jnp.concatenate([prev, tb], axis=2)

    kb = with_prev(k)
    vb = with_prev(v)
    validb = with_prev(valid[None])[0]
    q_idx = jnp.arange(SWA_BLOCK) + SWA_BLOCK
    k_idx = jnp.arange(2 * SWA_BLOCK)
    diff = q_idx[:, None] - k_idx[None, :]
    band = (diff >= 0) & (diff < SWA_WINDOW)
    mask = band[None] & validb[:, None, :]
    s = jnp.einsum('bnqhgd,bnkhd->bnhgqk', qb, kb).astype(jnp.float32) * (d ** -0.5)
    s = jnp.where(mask[None, :, None, None], s, MASK_VALUE)
    sink = jnp.broadcast_to(sinks.reshape(SWA_KV_HEADS, G).astype(jnp.float32)[None, None, :, :, None, None],
                            s.shape[:-1] + (1,))
    p = jax.nn.softmax(jnp.concatenate([s, sink], axis=-1), axis=-1)[..., :-1]
    o = jnp.einsum('bnhgqk,bnkhd->bnqhgd', p.astype(v.dtype), vb)
    return o.reshape(B, Lp, Hq * d)


def mla_attention(q_nope, q_rope, k_nope, k_rope, v, valid):
    B, Lp, H, _ = q_nope.shape
    nb = Lp // MLA_QUERY_BLOCK
    scale = (MLA_NOPE_DIM + MLA_ROPE_DIM) ** -0.5
    k_idx = jnp.arange(Lp)

    def block(i):
        start = i * MLA_QUERY_BLOCK
        qn = lax.dynamic_slice_in_dim(q_nope, start, MLA_QUERY_BLOCK, axis=1)
        qr = lax.dynamic_slice_in_dim(q_rope, start, MLA_QUERY_BLOCK, axis=1)
        s = (jnp.einsum('bqhd,bkhd->bhqk', qn, k_nope)
             + jnp.einsum('bqhd,bkd->bhqk', qr, k_rope)).astype(jnp.float32) * scale
        q_idx = start + jnp.arange(MLA_QUERY_BLOCK)
        mask = (k_idx[None, :] <= q_idx[:, None]) & valid[None, :]
        s = jnp.where(mask, s, MASK_VALUE)
        p = jax.nn.softmax(s, axis=-1).astype(v.dtype)
        return jnp.einsum('bhqk,bkhd->bqhd', p, v)

    o = lax.map(block, jnp.arange(nb))
    return jnp.transpose(o, (1, 0, 2, 3, 4)).reshape(B, Lp, H * MLA_V_DIM)


def attn_mixer(h, w_in, q_norm, w_uq, kv_norm, w_ukv, sinks, w_out):
    B, L, D = h.shape
    pad = (-L) % ATTN_ALIGN
    Lp = L + pad
    proj = jnp.pad(h @ w_in, ((0, 0), (pad, 0), (0, 0)))
    qa, ka, va, cq, ckv, kr = split_cols(proj, ATTN_SPLITS)
    valid = jnp.arange(Lp) >= pad
    o_a = swa_sink_attention(qa.reshape(B, Lp, SWA_HEADS, SWA_HEAD_DIM),
                             ka.reshape(B, Lp, SWA_KV_HEADS, SWA_HEAD_DIM),
                             va.reshape(B, Lp, SWA_KV_HEADS, SWA_HEAD_DIM), sinks, valid)
    qh = (rmsnorm(cq, q_norm) @ w_uq).reshape(B, Lp, MLA_HEADS, MLA_NOPE_DIM + MLA_ROPE_DIM)
    q_nope, q_rope = qh[..., :MLA_NOPE_DIM], qh[..., MLA_NOPE_DIM:]
    kvh = (rmsnorm(ckv, kv_norm) @ w_ukv).reshape(B, Lp, MLA_HEADS, MLA_NOPE_DIM + MLA_V_DIM)
    k_nope, v_b = kvh[..., :MLA_NOPE_DIM], kvh[..., MLA_NOPE_DIM:]
    cos, sin = rope_tables(Lp, pad)
    q_rope = apply_rope(q_rope, cos[:, None, :], sin[:, None, :])
    k_rope = apply_rope(kr, cos, sin)
    o_b = mla_attention(q_nope, q_rope, k_nope, k_rope, v_b, valid)
    o = jnp.concatenate([o_a, o_b], axis=-1)[:, pad:]
    return o @ w_out


def gla_mixer(h, w_in, w_gate, b_gate, out_norm, w_out):
    B, L, D = h.shape
    C = GLA_CHUNK
    pad = (-L) % C
    Lp = L + pad
    nc = Lp // C
    proj = jnp.pad(h @ w_in, ((0, 0), (pad, 0), (0, 0)))
    q, k, v, r, g1 = split_cols(proj, GLA_SPLITS)
    valid = (jnp.arange(Lp) >= pad).astype(jnp.float32)[None, :, None]
    log_a = jax.nn.log_sigmoid((g1 @ w_gate + b_gate).astype(jnp.float32)) / GLA_GATE_TAU
    qf = (q.astype(jnp.float32) * GLA_HEAD_K ** -0.5).reshape(B, nc, C, GLA_HEADS, GLA_HEAD_K)
    kf = (k.astype(jnp.float32) * valid).reshape(B, nc, C, GLA_HEADS, GLA_HEAD_K)
    vf = (v.astype(jnp.float32) * valid).reshape(B, nc, C, GLA_HEADS, GLA_HEAD_V)
    bcum = jnp.cumsum(log_a.reshape(B, nc, C, GLA_HEADS, GLA_HEAD_K), axis=2)
    q_dec = qf * jnp.exp(bcum)
    k_dec = kf * jnp.exp(-bcum)
    k_end = kf * jnp.exp(bcum[:, :, -1:] - bcum)
    chunk_decay = jnp.exp(bcum[:, :, -1])
    tri = jnp.tril(jnp.ones((C, C), dtype=bool))
    a = jnp.einsum('bnqhd,bnkhd->bnhqk', q_dec, k_dec)
    a = jnp.where(tri, a, 0.0)
    o_intra = jnp.einsum('bnhqk,bnkhe->bnqhe', a, vf)

    def step(state, inp):
        qd, ke, vc, dec = inp
        o = jnp.einsum('bqhd,bhde->bqhe', qd, state)
        state = dec[..., None] * state + jnp.einsum('bkhd,bkhe->bhde', ke, vc)
        return state, o

    state0 = jnp.zeros((B, GLA_HEADS, GLA_HEAD_K, GLA_HEAD_V), jnp.float32)
    xs = (jnp.moveaxis(q_dec, 1, 0), jnp.moveaxis(k_end, 1, 0),
          jnp.moveaxis(vf, 1, 0), jnp.moveaxis(chunk_decay, 1, 0))
    _, o_inter = lax.scan(step, state0, xs)
    o = (o_intra + jnp.moveaxis(o_inter, 0, 1)).reshape(B, Lp, GLA_HEADS, GLA_HEAD_V)
    o = rmsnorm(o, out_norm)
    rg = jax.nn.silu(r.astype(jnp.float32)).reshape(B, Lp, GLA_HEADS, GLA_HEAD_V)
    o = (o * rg).reshape(B, Lp, GLA_VALUE_DIM)[:, pad:].astype(h.dtype)
    return o @ w_out


def peer_ffn(h, w_query, sub_keys, expert_u, expert_v):
    B, L, D = h.shape
    T = B * L
    tb = PEER_TOKEN_BLOCK
    n_blocks = -(-T // tb)
    flat = jnp.pad(h.reshape(T, D), ((0, n_blocks * tb - T), (0, 0))).reshape(n_blocks, tb, D)
    k = PEER_TOPK

    def block(xb):
        q = (xb @ w_query).reshape(tb, PEER_HEADS, 2, PEER_KEY_DIM // 2)
        s = jnp.einsum('thpd,pnd->thpn', q, sub_keys).astype(jnp.float32)
        top_s, top_i = lax.top_k(s, k)
        cand_s = top_s[:, :, 0, :, None] + top_s[:, :, 1, None, :]
        cand_i = top_i[:, :, 0, :, None] * PEER_N_KEYS + top_i[:, :, 1, None, :]
        best_s, best_j = lax.top_k(cand_s.reshape(tb, PEER_HEADS, k * k), k)
        idx = jnp.take_along_axis(cand_i.reshape(tb, PEER_HEADS, k * k), best_j, axis=-1)
        gate = jax.nn.softmax(best_s, axis=-1)
        u = expert_u[idx]
        act = jax.nn.gelu(jnp.einsum('thkd,td->thk', u, xb).astype(jnp.float32))
        w = (gate * act).astype(xb.dtype)
        return jnp.einsum('thk,thkd->td', w, expert_v[idx])

    out = lax.map(block, flat).reshape(n_blocks * tb, D)[:T]
    return out.reshape(B, L, D)


def setup_inputs(seed: int = 0) -> dict:
    key = jax.random.key(seed)
    ks = jax.random.split(key, 22)

    def nrm(i, shape, scale):
        return jax.random.normal(ks[i], shape, jnp.float32) * scale

    NA, NG = N_ATTN_LAYERS, N_GLA_LAYERS
    return {
        "x": nrm(0, (BATCH, SEQ, D_MODEL), 1.0),
        "meta_tokens": nrm(1, (N_META, D_MODEL), 1.0),
        "attn_norm": 1.0 + nrm(2, (NA, D_MODEL), 0.02),
        "attn_w_in": nrm(3, (NA, D_MODEL, ATTN_IN), D_MODEL ** -0.5),
        "mla_q_norm": 1.0 + nrm(4, (NA, MLA_Q_LORA), 0.02),
        "mla_w_uq": nrm(5, (NA, MLA_Q_LORA, MLA_HEADS * (MLA_NOPE_DIM + MLA_ROPE_DIM)), MLA_Q_LORA ** -0.5),
        "mla_kv_norm": 1.0 + nrm(6, (NA, MLA_KV_LORA), 0.02),
        "mla_w_ukv": nrm(7, (NA, MLA_KV_LORA, MLA_HEADS * (MLA_NOPE_DIM + MLA_V_DIM)), MLA_KV_LORA ** -0.5),
        "swa_sinks": nrm(8, (NA, SWA_HEADS), 0.5),
        "attn_w_out": nrm(9, (NA, ATTN_MIX, D_MODEL), ATTN_MIX ** -0.5),
        "gla_norm": 1.0 + nrm(10, (NG, D_MODEL), 0.02),
        "gla_w_in": nrm(11, (NG, D_MODEL, GLA_IN), D_MODEL ** -0.5),
        "gla_w_gate": nrm(12, (NG, GLA_GATE_RANK, GLA_KEY_DIM), GLA_GATE_RANK ** -0.5),
        "gla_b_gate": nrm(13, (NG, GLA_KEY_DIM), 0.1),
        "gla_out_norm": 1.0 + nrm(14, (NG, GLA_HEAD_V), 0.02),
        "gla_w_out": nrm(15, (NG, GLA_VALUE_DIM, D_MODEL), GLA_VALUE_DIM ** -0.5),
        "ffn_norm": 1.0 + nrm(16, (DEPTH, D_MODEL), 0.02),
        "peer_w_query": nrm(17, (DEPTH, D_MODEL, PEER_HEADS * PEER_KEY_DIM), D_MODEL ** -0.5),
        "peer_sub_keys": nrm(18, (DEPTH, 2, PEER_N_KEYS, PEER_KEY_DIM // 2), (PEER_KEY_DIM // 2) ** -0.5),
        "peer_u": nrm(19, (DEPTH, PEER_EXPERTS, D_MODEL), D_MODEL ** -0.5),
        "peer_v": nrm(20, (DEPTH, PEER_EXPERTS, D_MODEL), (PEER_HEADS * PEER_TOPK) ** -0.5),
        "final_norm": 1.0 + nrm(21, (D_MODEL,), 0.02),
    }


def reference(x, meta_tokens, attn_norm, attn_w_in, mla_q_norm, mla_w_uq, mla_kv_norm, mla_w_ukv,
              swa_sinks, attn_w_out, gla_norm, gla_w_in, gla_w_gate, gla_b_gate, gla_out_norm, gla_w_out,
              ffn_norm, peer_w_query, peer_sub_keys, peer_u, peer_v, final_norm):
    B = x.shape[0]
    meta = jnp.broadcast_to(meta_tokens.astype(x.dtype)[None], (B, N_META, x.shape[-1]))
    h = jnp.concatenate([meta, x], axis=1)
    for layer in range(DEPTH):
        i = layer // 2
        if layer % 2 == 0:
            h = h + attn_mixer(rmsnorm(h, attn_norm[i]), attn_w_in[i], mla_q_norm[i], mla_w_uq[i],
                               mla_kv_norm[i], mla_w_ukv[i], swa_sinks[i], attn_w_out[i])
        else:
            h = h + gla_mixer(rmsnorm(h, gla_norm[i]), gla_w_in[i], gla_w_gate[i], gla_b_gate[i],
                              gla_out_norm[i], gla_w_out[i])
        h = h + peer_ffn(rmsnorm(h, ffn_norm[layer]), peer_w_query[layer], peer_sub_keys[layer],
                         peer_u[layer], peer_v[layer])
    return rmsnorm(h, final_norm)[:, N_META:]
```

```python
import functools
import math

import jax
import jax.numpy as jnp
import numpy as np
from jax import lax
from jax.experimental import pallas as pl
from jax.experimental.pallas import tpu as pltpu

F32 = jnp.float32
BF16 = jnp.bfloat16

LANES = 128
EPS = 1e-6
MASK_VALUE = -1e30
N_META = 16

SWA_HEADS = 8
SWA_KV_HEADS = 2
SWA_HEAD_DIM = 64
SWA_BLOCK = 128

MLA_HEADS = 8
MLA_Q_LORA = 384
MLA_KV_LORA = 256
MLA_NOPE = 64
MLA_ROPE = 32
MLA_V = 64
ROPE_THETA = 10000.0

GLA_HEADS = 4
GLA_HEAD_K = 128
GLA_HEAD_V = 256
GLA_RANK = 16
GLA_TAU = 16.0
GLA_CHUNK = 64

PEER_KEYS = 128
PEER_HEADS = 8
PEER_TOPK = 16
PEER_HALF = 64

VMEM_LIMIT = 56 * 1024 * 1024


def _pick(n, candidates):
    for c in candidates:
        if n % c == 0:
            return c
    raise ValueError(f"no tile for {n} in {candidates}")


def _rms(x, g):
    return x * lax.rsqrt(jnp.mean(x * x, axis=-1, keepdims=True) + EPS) * g


def _dot(a, b):
    return jnp.dot(a, b, preferred_element_type=F32)


def _dot_nt(a, b):
    return lax.dot_general(a, b, (((1,), (1,)), ((), ())), preferred_element_type=F32)


def _cparams(sem):
    return pltpu.CompilerParams(dimension_semantics=sem, vmem_limit_bytes=VMEM_LIMIT)


def _rope(x, c, s):
    n = x.shape[-1]
    up = pltpu.roll(x, n - 16, axis=1)
    down = pltpu.roll(x, 16, axis=1)
    lane = lax.broadcasted_iota(jnp.int32, x.shape, 1) % LANES
    swapped = jnp.where(lane < 80, up, down)
    return x * c + swapped * s


def _attn_in_kernel(h_ref, g_ref, win_ref, qn_ref, wuq_ref, kvn_ref, wk_ref, wv_ref, c_ref, s_ref,
                    qa_ref, ka_ref, va_ref, q_ref, k_ref, v_ref):
    xn = _rms(h_ref[...], g_ref[...])
    proj = _dot(xn.astype(BF16), win_ref[...])
    qa_ref[...] = (proj[:, 0:1024] * (SWA_HEAD_DIM ** -0.5)).astype(BF16)
    ka_ref[...] = proj[:, 1024:1280].astype(BF16)
    va_ref[...] = proj[:, 1280:1408].astype(BF16)
    cqn = _rms(proj[:, 1408:1792], qn_ref[...])
    ckvn = _rms(proj[:, 1792:2048], kvn_ref[...]).astype(BF16)
    kr = proj[:, 2048:2176]
    c = c_ref[...]
    s = s_ref[...]
    c8 = jnp.concatenate([c] * MLA_HEADS, axis=1)
    s8 = jnp.concatenate([s] * MLA_HEADS, axis=1)
    q_all = _dot(cqn.astype(BF16), wuq_ref[...])
    scale = (MLA_NOPE + MLA_ROPE) ** -0.5
    q_ref[...] = (_rope(q_all, c8, s8) * scale).astype(BF16)
    kr = _rope(kr, c, s)
    k_all = _dot(ckvn, wk_ref[...]) + jnp.concatenate([kr] * MLA_HEADS, axis=1)
    k_ref[...] = k_all.astype(BF16)
    v_ref[...] = _dot(ckvn, wv_ref[...]).astype(BF16)


def _attn_in(h, norm, w_in_p, q_norm, w_uq_p, kv_norm, w_k_p, w_v_p, rope_c, rope_s, tm):
    B, Lp, D = h.shape
    nt = Lp // tm
    row = lambda w: pl.BlockSpec((None, tm, w), lambda b, i: (b, i, 0))
    full = lambda a: pl.BlockSpec(a.shape, lambda b, i: (0,) * a.ndim)
    tab = pl.BlockSpec((tm, LANES), lambda b, i: (i, 0))
    outs = [(1024, BF16), (256, BF16), (128, BF16), (1024, BF16), (1024, BF16), (512, BF16)]
    return pl.pallas_call(
        _attn_in_kernel,
        grid=(B, nt),
        in_specs=[row(D), full(norm), full(w_in_p), full(q_norm), full(w_uq_p), full(kv_norm),
                  full(w_k_p), full(w_v_p), tab, tab],
        out_specs=[row(w) for w, _ in outs],
        out_shape=[jax.ShapeDtypeStruct((B, Lp, w), dt) for w, dt in outs],
        compiler_params=_cparams(("parallel", "parallel")),
        name="attn_in",
    )(h, norm, w_in_p, q_norm, w_uq_p, kv_norm, w_k_p, w_v_p, rope_c, rope_s)


def _swa_kernel(sink_ref, q_ref, kp_ref, kc_ref, vp_ref, vc_ref, o_ref, *, pad):
    n = pl.program_id(1)
    bl = SWA_BLOCK
    k2 = jnp.concatenate([kp_ref[...], kc_ref[...]], axis=0)
    v2 = jnp.concatenate([vp_ref[...], vc_ref[...]], axis=0)
    qi = lax.broadcasted_iota(jnp.int32, (bl, 2 * bl), 0) + bl
    ki = lax.broadcasted_iota(jnp.int32, (bl, 2 * bl), 1)
    diff = qi - ki
    kpos = (n - 1) * bl + ki
    ok = jnp.where(diff >= 0, jnp.where(diff < bl, jnp.where(kpos >= pad, 1, 0), 0), 0)
    lane = lax.broadcasted_iota(jnp.int32, (bl, LANES), 1)
    outs = []
    for hq in range(SWA_HEADS):
        g = hq // (SWA_HEADS // SWA_KV_HEADS)
        s = _dot_nt(q_ref[:, hq * LANES:(hq + 1) * LANES], k2[:, g * LANES:(g + 1) * LANES])
        s = jnp.where(ok > 0, s, MASK_VALUE)
        sink = sink_ref[hq]
        m = jnp.maximum(jnp.max(s, axis=-1, keepdims=True), sink)
        p = jnp.exp(s - m)
        l = jnp.sum(p, axis=-1, keepdims=True) + jnp.exp(sink - m)
        outs.append(_dot(p.astype(BF16), v2) / l)
    half = SWA_HEADS // 2
    for m_ in range(half):
        o_ref[:, m_ * LANES:(m_ + 1) * LANES] = jnp.where(
            lane < SWA_HEAD_DIM, outs[m_], outs[m_ + half]).astype(BF16)


def _swa(qa, ka, va, sinks, pad):
    B, Lp, _ = qa.shape
    nb = Lp // SWA_BLOCK
    cur = lambda w: pl.BlockSpec((None, SWA_BLOCK, w), lambda b, n: (b, n, 0))
    prev = lambda w: pl.BlockSpec((None, SWA_BLOCK, w), lambda b, n: (b, jnp.maximum(n - 1, 0), 0))
    return pl.pallas_call(
        functools.partial(_swa_kernel, pad=pad),
        grid=(B, nb),
        in_specs=[pl.BlockSpec(memory_space=pltpu.SMEM), cur(1024), prev(256), cur(256), prev(128), cur(128)],
        out_specs=cur(512),
        out_shape=jax.ShapeDtypeStruct((B, Lp, 512), BF16),
        compiler_params=_cparams(("parallel", "parallel")),
        name="swa",
    )(sinks, qa, ka, ka, va, va)


def _mla_kernel(q_ref, k_ref, v_ref, o_ref, m_ref, l_ref, acc_ref, *, tq, tk, pad):
    i = pl.program_id(2)
    m_ref[...] = jnp.full(m_ref.shape, -jnp.inf, F32)
    l_ref[...] = jnp.zeros(l_ref.shape, F32)
    acc_ref[...] = jnp.zeros(acc_ref.shape, F32)
    qrow = i * tq + lax.broadcasted_iota(jnp.int32, (tq, tk), 0)
    kcol = lax.broadcasted_iota(jnp.int32, (tq, tk), 1)

    def body(kb, carry):
        ks = pl.multiple_of(kb * tk, tk)
        kblk = k_ref[pl.ds(ks, tk), :]
        vblk = v_ref[pl.ds(ks, tk), :]
        kpos = kcol + kb * tk
        ok = jnp.where(kpos <= qrow, jnp.where(kpos >= pad, 1, 0), 0)
        for hh in range(2):
            s = _dot_nt(q_ref[:, hh * LANES:(hh + 1) * LANES], kblk[:, hh * LANES:(hh + 1) * LANES])
            s = jnp.where(ok > 0, s, MASK_VALUE)
            m_old = m_ref[hh]
            m_new = jnp.maximum(m_old, jnp.max(s, axis=-1, keepdims=True))
            alpha = jnp.exp(m_old - m_new)
            p = jnp.exp(s - m_new)
            l_ref[hh] = alpha * l_ref[hh] + jnp.sum(p, axis=-1, keepdims=True)
            acc_ref[hh] = alpha * acc_ref[hh] + _dot(p.astype(BF16), vblk)
            m_ref[hh] = m_new
        return carry

    nkb = (i * tq + tq + tk - 1) // tk
    lax.fori_loop(0, nkb, body, 0)
    lane = lax.broadcasted_iota(jnp.int32, (tq, LANES), 1)
    o0 = acc_ref[0] / l_ref[0]
    o1 = acc_ref[1] / l_ref[1]
    o_ref[...] = jnp.where(lane < MLA_V, o0, o1).astype(BF16)


def _mla(q, k, v, pad, tq):
    B, Lp, _ = q.shape
    nq = Lp // tq
    pairs = MLA_HEADS // 2
    return pl.pallas_call(
        functools.partial(_mla_kernel, tq=tq, tk=tq, pad=pad),
        grid=(B, pairs, nq),
        in_specs=[pl.BlockSpec((None, tq, 2 * LANES), lambda b, j, i: (b, i, j)),
                  pl.BlockSpec((None, Lp, 2 * LANES), lambda b, j, i: (b, 0, j)),
                  pl.BlockSpec((None, Lp, LANES), lambda b, j, i: (b, 0, j))],
        out_specs=pl.BlockSpec((None, tq, LANES), lambda b, j, i: (b, i, j)),
        out_shape=jax.ShapeDtypeStruct((B, Lp, pairs * LANES), BF16),
        scratch_shapes=[pltpu.VMEM((2, tq, 1), F32), pltpu.VMEM((2, tq, 1), F32),
                        pltpu.VMEM((2, tq, LANES), F32)],
        compiler_params=_cparams(("parallel", "parallel", "arbitrary")),
        name="mla",
    )(q, k, v)


def _gla_in_kernel(h_ref, g_ref, win_ref, wg_ref, bg_ref, q_ref, k_ref, v_ref, r_ref, la_ref, *, tm, pad):
    i = pl.program_id(1)
    xn = _rms(h_ref[...], g_ref[...])
    proj = _dot(xn.astype(BF16), win_ref[...])
    pos = i * tm + lax.broadcasted_iota(jnp.int32, (tm, 1), 0)
    valid = jnp.where(pos >= pad, 1.0, 0.0)
    q_ref[...] = (proj[:, 0:512] * (GLA_HEAD_K ** -0.5)).astype(BF16)
    k_ref[...] = (proj[:, 512:1024] * valid).astype(BF16)
    v_ref[...] = (proj[:, 1024:2048] * valid).astype(BF16)
    r_ref[...] = proj[:, 2048:3072].astype(BF16)
    gate = _dot(proj[:, 3072:3200].astype(BF16), wg_ref[...]) + bg_ref[...]
    log_sig = jnp.minimum(gate, 0.0) - jnp.log1p(jnp.exp(-jnp.abs(gate)))
    la_ref[...] = log_sig / GLA_TAU


def _gla_in(h, norm, w_in_p, w_gate_p, b_gate, pad, tm):
    B, Lp, D = h.shape
    row = lambda w: pl.BlockSpec((None, tm, w), lambda b, i: (b, i, 0))
    full = lambda a: pl.BlockSpec(a.shape, lambda b, i: (0,) * a.ndim)
    outs = [(512, BF16), (512, BF16), (1024, BF16), (1024, BF16), (512, F32)]
    return pl.pallas_call(
        functools.partial(_gla_in_kernel, tm=tm, pad=pad),
        grid=(B, Lp // tm),
        in_specs=[row(D), full(norm), full(w_in_p), full(w_gate_p), full(b_gate)],
        out_specs=[row(w) for w, _ in outs],
        out_shape=[jax.ShapeDtypeStruct((B, Lp, w), dt) for w, dt in outs],
        compiler_params=_cparams(("parallel", "parallel")),
        name="gla_in",
    )(h, norm, w_in_p, w_gate_p, b_gate)


def _gla_kernel(tri_ref, q_ref, k_ref, v_ref, r_ref, la_ref, on_ref, o_ref, st_ref, *, tb):
    @pl.when(pl.program_id(2) == 0)
    def _():
        st_ref[...] = jnp.zeros(st_ref.shape, F32)

    C = GLA_CHUNK
    tri = tri_ref[...]
    low = (lax.broadcasted_iota(jnp.int32, (C, C), 0) >= lax.broadcasted_iota(jnp.int32, (C, C), 1))
    for c2 in range(tb // (2 * C)):
        la = la_ref[pl.ds(c2 * 2 * C, 2 * C), :]
        la_hi = la.astype(BF16)
        la_lo = (la - la_hi.astype(F32)).astype(BF16)
        bcum2 = _dot(tri, la_hi) + _dot(tri, la_lo)
        for cc in range(2):
            r0 = c2 * 2 * C + cc * C
            b = bcum2[cc * C:(cc + 1) * C]
            bend = b[C - 1:C]
            q = q_ref[pl.ds(r0, C), :].astype(F32)
            k = k_ref[pl.ds(r0, C), :].astype(F32)
            v = v_ref[pl.ds(r0, C), :]
            qd = (q * jnp.exp(b)).astype(BF16)
            kd = (k * jnp.exp(-b)).astype(BF16)
            ke = (k * jnp.exp(bend - b)).astype(BF16)
            a = jnp.where(low, _dot_nt(qd, kd), 0.0)
            st = st_ref[...]
            o = _dot(a.astype(BF16), v) + _dot_nt(qd, st.astype(BF16))
            kv_t = lax.dot_general(v, ke, (((0,), (0,)), ((), ())), preferred_element_type=F32)
            st_ref[...] = st * jnp.exp(bend) + kv_t
            o = _rms(o, on_ref[...])
            rg = r_ref[pl.ds(r0, C), :].astype(F32)
            o_ref[pl.ds(r0, C), :] = (o * (rg * jax.nn.sigmoid(rg))).astype(BF16)


def _gla(q, k, v, r, la, out_norm, tb):
    B, Lp, _ = q.shape
    C = GLA_CHUNK
    idx = np.arange(2 * C)
    tri = jnp.asarray(((idx[:, None] >= idx[None, :]) & (idx[:, None] // C == idx[None, :] // C)), BF16)
    blk = lambda w: pl.BlockSpec((None, tb, w), lambda b, hh, i: (b, i, hh))
    return pl.pallas_call(
        functools.partial(_gla_kernel, tb=tb),
        grid=(B, GLA_HEADS, Lp // tb),
        in_specs=[pl.BlockSpec(tri.shape, lambda b, hh, i: (0, 0)),
                  blk(GLA_HEAD_K), blk(GLA_HEAD_K), blk(GLA_HEAD_V), blk(GLA_HEAD_V), blk(GLA_HEAD_K),
                  pl.BlockSpec(out_norm.shape, lambda b, hh, i: (0, 0))],
        out_specs=blk(GLA_HEAD_V),
        out_shape=jax.ShapeDtypeStruct((B, Lp, GLA_HEADS * GLA_HEAD_V), BF16),
        scratch_shapes=[pltpu.VMEM((GLA_HEAD_V, GLA_HEAD_K), F32)],
        compiler_params=_cparams(("parallel", "parallel", "arbitrary")),
        name="gla",
    )(tri, q, k, v, r, la, out_norm)


def _mix_out_kernel(*refs, n_mix):
    h_ref = refs[0]
    o_refs = refs[1:1 + n_mix]
    w_refs = refs[1 + n_mix:1 + 2 * n_mix]
    g_ref, wqs_ref, h1_ref, xn_ref, st_ref = refs[1 + 2 * n_mix:]
    h1 = h_ref[...]
    for o_ref, w_ref in zip(o_refs, w_refs):
        h1 = h1 + _dot(o_ref[...], w_ref[...])
    h1_ref[...] = h1
    xn = _rms(h1, g_ref[...]).astype(BF16)
    xn_ref[...] = xn
    st_ref[...] = _dot_nt(wqs_ref[...], xn)


def _mix_out(h, outs, ws, ffn_norm, wqs_t, tm):
    B, Lp, D = h.shape
    nt = Lp // tm
    n_mix = len(outs)
    row = lambda w: pl.BlockSpec((None, tm, w), lambda b, i: (b, i, 0))
    full = lambda a: pl.BlockSpec(a.shape, lambda b, i: (0,) * a.ndim)
    nsc = wqs_t.shape[0]
    return pl.pallas_call(
        functools.partial(_mix_out_kernel, n_mix=n_mix),
        grid=(B, nt),
        in_specs=[row(D)] + [row(o.shape[-1]) for o in outs] + [full(w) for w in ws]
                 + [full(ffn_norm), full(wqs_t)],
        out_specs=[row(D), row(D), pl.BlockSpec((nsc, tm), lambda b, i: (0, b * nt + i))],
        out_shape=[jax.ShapeDtypeStruct((B, Lp, D), F32), jax.ShapeDtypeStruct((B, Lp, D), BF16),
                   jax.ShapeDtypeStruct((nsc, B * Lp), F32)],
        compiler_params=_cparams(("parallel", "parallel")),
        name="mix_out",
    )(h, *outs, *ws, ffn_norm, wqs_t)


def _wqs_kernel(sk_ref, wq_ref, o_ref):
    o_ref[...] = lax.dot_general(sk_ref[...], wq_ref[...], (((1,), (1,)), ((), ())),
                                 preferred_element_type=F32, precision=lax.Precision.HIGHEST)


def _peer_score_weights(w_query, sub_keys):
    D = w_query.shape[0]
    z = jnp.zeros((PEER_KEYS, PEER_HALF), F32)
    sk_pair_t = jnp.concatenate([jnp.concatenate([sub_keys[0], z], axis=1),
                                 jnp.concatenate([z, sub_keys[1]], axis=1)], axis=0)
    return pl.pallas_call(
        _wqs_kernel,
        grid=(PEER_HEADS,),
        in_specs=[pl.BlockSpec(sk_pair_t.shape, lambda hh: (0, 0)),
                  pl.BlockSpec((D, 2 * PEER_HALF), lambda hh: (0, hh))],
        out_specs=pl.BlockSpec((2 * PEER_KEYS, D), lambda hh: (hh, 0)),
        out_shape=jax.ShapeDtypeStruct((PEER_HEADS * 2 * PEER_KEYS, D), F32),
        compiler_params=_cparams(("parallel",)),
        name="peer_score_weights",
    )(sk_pair_t, w_query)


def _hyperbola():
    return [(a, b) for a in range(PEER_TOPK) for b in range(PEER_TOPK) if (a + 1) * (b + 1) <= PEER_TOPK]


def _topk_kernel(s_ref, n_ref, e1_ref, r2_ref, e2_ref, rank_ref, top_ref, na_ref, cand_ref, *, tl):
    K = PEER_TOPK
    NK = PEER_KEYS
    key = lax.broadcasted_iota(jnp.int32, (NK, tl), 0).astype(F32)
    for hp in range(2 * PEER_HEADS):
        hh, p = divmod(hp, 2)

        def extract(a, carry, hh=hh, p=p):
            s, rank = carry
            m = jnp.max(s, axis=0, keepdims=True)
            idx = jnp.min(jnp.where(s == m, key, float(NK)), axis=0, keepdims=True)
            sel = key == idx
            top_ref[p, a, pl.ds(hh, 1), :] = m
            return jnp.where(sel, -jnp.inf, s), jnp.where(sel, a.astype(F32), rank)

        _, rank = lax.fori_loop(0, K, extract, (s_ref[pl.ds(hp * NK, NK), :], jnp.full((NK, tl), float(K), F32)))
        rank_ref[hp] = rank
    cand = _hyperbola()
    for ci, (a, b) in enumerate(cand):
        cand_ref[ci] = top_ref[0, a] + top_ref[1, b]
    for a in range(K):
        na_ref[a] = jnp.zeros((PEER_HEADS, tl), F32)
    cmax = cand_ref[0]
    big = float(K * K)

    def pick(_, z):
        vals = [cand_ref[ci] for ci in range(len(cand))]
        m = vals[0]
        for v in vals[1:]:
            m = jnp.maximum(m, v)
        idx = jnp.full((PEER_HEADS, tl), big, F32)
        for (a, b), v in zip(cand, vals):
            idx = jnp.minimum(idx, jnp.where(v == m, float(a * K + b), big))
        for ci, (a, b) in enumerate(cand):
            sel = idx == float(a * K + b)
            na_ref[a] += jnp.where(sel, 1.0, 0.0)
            cand_ref[ci] = jnp.where(sel, -jnp.inf, vals[ci])
        return z + jnp.exp(m - cmax)

    z = lax.fori_loop(0, K, pick, jnp.zeros((PEER_HEADS, tl), F32))
    na_ref[K] = 1.0 / z
    for hh in range(PEER_HEADS):
        r1 = rank_ref[2 * hh]
        n = jnp.zeros((NK, tl), F32)
        for a in range(K):
            n = n + jnp.where(r1 == float(a), na_ref[a, pl.ds(hh, 1), :], 0.0)
        n_ref[hh] = n
        s1 = s_ref[pl.ds((2 * hh) * NK, NK), :]
        s2 = s_ref[pl.ds((2 * hh + 1) * NK, NK), :]
        e1_ref[hh] = jnp.exp(s1 - top_ref[0, 0, pl.ds(hh, 1), :]) * na_ref[K, pl.ds(hh, 1), :]
        e2_ref[hh] = jnp.exp(s2 - top_ref[1, 0, pl.ds(hh, 1), :]).astype(BF16)
        r2_ref[hh] = rank_ref[2 * hh + 1].astype(BF16)


def _topk(scores_t, tl):
    nsc, R = scores_t.shape
    H, NK, K = PEER_HEADS, PEER_KEYS, PEER_TOPK
    blk = pl.BlockSpec((H, NK, tl), lambda t: (0, 0, t))
    return pl.pallas_call(
        functools.partial(_topk_kernel, tl=tl),
        grid=(R // tl,),
        in_specs=[pl.BlockSpec((nsc, tl), lambda t: (0, t))],
        out_specs=[blk, blk, blk, blk],
        out_shape=[jax.ShapeDtypeStruct((H, NK, R), F32), jax.ShapeDtypeStruct((H, NK, R), F32),
                   jax.ShapeDtypeStruct((H, NK, R), BF16), jax.ShapeDtypeStruct((H, NK, R), BF16)],
        scratch_shapes=[pltpu.VMEM((2 * H, NK, tl), F32), pltpu.VMEM((2, K, H, tl), F32),
                        pltpu.VMEM((K + 1, H, tl), F32), pltpu.VMEM((len(_hyperbola()), H, tl), F32)],
        compiler_params=_cparams(("parallel",)),
        name="peer_topk",
    )(scores_t)


def _gelu_tanh(x):
    return 0.5 * x * (1.0 + jnp.tanh(math.sqrt(2.0 / math.pi) * (x + 0.044715 * (x * x * x))))


def _peer_kernel(xn_ref, u_ref, vt_ref, n_ref, e1_ref, r2_ref, e2_ref, h1_ref, o_ref, acc_ref, w_ref,
                 *, ec, tl):
    c = pl.program_id(1)

    @pl.when(c == 0)
    def _():
        acc_ref[...] = jnp.zeros(acc_ref.shape, F32)

    act_t = _dot_nt(u_ref[...], xn_ref[...])
    rows = ec // PEER_KEYS
    for ii in range(rows):
        i = c * rows + ii
        g = jnp.zeros((PEER_KEYS, tl), BF16)
        for hh in range(PEER_HEADS):
            n_i = n_ref[hh, pl.ds(i, 1), :].astype(BF16)
            e1_i = e1_ref[hh, pl.ds(i, 1), :].astype(BF16)
            g = g + jnp.where(r2_ref[hh] < n_i, e2_ref[hh], jnp.zeros((), BF16)) * e1_i
        a = _gelu_tanh(act_t[ii * PEER_KEYS:(ii + 1) * PEER_KEYS])
        w_ref[pl.ds(ii * PEER_KEYS, PEER_KEYS), :] = g * a.astype(BF16)
    acc_ref[...] += _dot(vt_ref[...], w_ref[...])

    @pl.when(c == pl.num_programs(1) - 1)
    def _():
        o_ref[...] = h1_ref[...] + acc_ref[...].T


def _peer(xn, u, vt, sel, h1, tl, ec):
    R, D = xn.shape
    E = u.shape[0]
    n, e1, r2, e2 = sel
    H, NK = PEER_HEADS, PEER_KEYS
    selblk = pl.BlockSpec((H, NK, tl), lambda t, c: (0, 0, t))
    rowblk = pl.BlockSpec((tl, D), lambda t, c: (t, 0))
    return pl.pallas_call(
        functools.partial(_peer_kernel, ec=ec, tl=tl),
        grid=(R // tl, E // ec),
        in_specs=[rowblk, pl.BlockSpec((ec, D), lambda t, c: (c, 0)), pl.BlockSpec((D, ec), lambda t, c: (0, c)),
                  selblk, selblk, selblk, selblk, rowblk],
        out_specs=rowblk,
        out_shape=jax.ShapeDtypeStruct((R, D), F32),
        scratch_shapes=[pltpu.VMEM((D, tl), F32), pltpu.VMEM((ec, tl), BF16)],
        compiler_params=_cparams(("parallel", "arbitrary")),
        name="peer_dense",
    )(xn, u, vt, n, e1, r2, e2, h1)


def _final_kernel(h_ref, g_ref, o_ref):
    o_ref[...] = _rms(h_ref[...], g_ref[...])


def _final_norm(h, g, skip_blocks, tm):
    B, Lp, D = h.shape
    nout = Lp // tm - skip_blocks
    return pl.pallas_call(
        _final_kernel,
        grid=(B, nout),
        in_specs=[pl.BlockSpec((None, tm, D), lambda b, i: (b, i + skip_blocks, 0)),
                  pl.BlockSpec(g.shape, lambda b, i: (0, 0))],
        out_specs=pl.BlockSpec((None, tm, D), lambda b, i: (b, i, 0)),
        out_shape=jax.ShapeDtypeStruct((B, nout * tm, D), F32),
        compiler_params=_cparams(("parallel", "parallel")),
        name="final_norm",
    )(h, g)


def _pad_heads(w, n_heads, width, out_width=LANES):
    k = w.shape[0]
    w = w.reshape(k, n_heads, width)
    return jnp.pad(w, ((0, 0), (0, 0), (0, out_width - width))).reshape(k, n_heads * out_width)


def _attn_weights(w_in, w_uq, w_ukv, w_out):
    D = w_in.shape[0]
    qa, ka, va, cq, ckv, kr = jnp.split(w_in, list(np.cumsum([512, 128, 128, MLA_Q_LORA, MLA_KV_LORA])), axis=1)
    zeros = lambda n: jnp.zeros((D, n), w_in.dtype)
    w_in_p = jnp.concatenate([_pad_heads(qa, SWA_HEADS, SWA_HEAD_DIM), _pad_heads(ka, SWA_KV_HEADS, SWA_HEAD_DIM),
                              va, cq, ckv, zeros(64), kr, zeros(32)], axis=1).astype(BF16)
    w_uq_p = _pad_heads(w_uq, MLA_HEADS, MLA_NOPE + MLA_ROPE).astype(BF16)
    kv = w_ukv.reshape(MLA_KV_LORA, MLA_HEADS, MLA_NOPE + MLA_V)
    w_k_p = jnp.pad(kv[:, :, :MLA_NOPE], ((0, 0), (0, 0), (0, LANES - MLA_NOPE))).reshape(MLA_KV_LORA, -1).astype(BF16)
    w_v_p = kv[:, :, MLA_NOPE:].reshape(MLA_KV_LORA, -1).astype(BF16)
    wa = w_out[:SWA_HEADS * SWA_HEAD_DIM].reshape(2, SWA_HEADS // 2, SWA_HEAD_DIM, -1)
    w_out_a = jnp.transpose(wa, (1, 0, 2, 3)).reshape(SWA_HEADS * SWA_HEAD_DIM, -1).astype(BF16)
    w_out_b = w_out[SWA_HEADS * SWA_HEAD_DIM:].astype(BF16)
    return w_in_p, w_uq_p, w_k_p, w_v_p, w_out_a, w_out_b


def _rope_tables(Lp, pad):
    pos = (jnp.arange(Lp) - pad).astype(F32)
    inv = ROPE_THETA ** (-jnp.arange(0, MLA_ROPE, 2, dtype=F32) / MLA_ROPE)
    ang = pos[:, None] * inv[None, :]
    cos, sin = jnp.cos(ang), jnp.sin(ang)
    one = jnp.ones((Lp, MLA_NOPE), F32)
    zero = jnp.zeros((Lp, MLA_NOPE), F32)
    c = jnp.concatenate([one, cos, cos, one[:, :32]], axis=1)
    s = jnp.concatenate([zero, -sin, sin, zero[:, :32]], axis=1)
    return c, s


def _peer_layer(h, outs, ws, ffn_norm, w_query, sub_keys, u, v, tm, tl, ec):
    B, Lp, D = h.shape
    wqs_t = _peer_score_weights(w_query, sub_keys).astype(BF16)
    h1, xn, scores_t = _mix_out(h, outs, ws, ffn_norm[None], wqs_t, tm)
    sel = _topk(scores_t, tl)
    h2 = _peer(xn.reshape(B * Lp, D), u.astype(BF16), v.T.astype(BF16), sel, h1.reshape(B * Lp, D), tl, ec)
    return h2.reshape(B, Lp, D)


def kernel(x, meta_tokens, attn_norm, attn_w_in, mla_q_norm, mla_w_uq, mla_kv_norm, mla_w_ukv, swa_sinks,
           attn_w_out, gla_norm, gla_w_in, gla_w_gate, gla_b_gate, gla_out_norm, gla_w_out, ffn_norm,
           peer_w_query, peer_sub_keys, peer_u, peer_v, final_norm):
    B, S, D = x.shape
    L = S + N_META
    pad = (-L) % SWA_BLOCK
    Lp = L + pad
    tm = _pick(Lp, (640, 512, 384, 256, 128))
    tl = _pick(B * Lp, (512, 256, 128))
    ec = 1024

    meta = jnp.broadcast_to(meta_tokens.astype(x.dtype)[None], (B, N_META, D))
    h = jnp.concatenate([jnp.zeros((B, pad, D), x.dtype), meta, x], axis=1)

    w_in_p, w_uq_p, w_k_p, w_v_p, w_out_a, w_out_b = _attn_weights(attn_w_in[0], mla_w_uq[0], mla_w_ukv[0],
                                                                   attn_w_out[0])
    rope_c, rope_s = _rope_tables(Lp, pad)
    qa, ka, va, q, k, v = _attn_in(h, attn_norm[0][None], w_in_p, mla_q_norm[0][None], w_uq_p,
                                   mla_kv_norm[0][None], w_k_p, w_v_p, rope_c, rope_s, tm)
    o_a = _swa(qa, ka, va, swa_sinks[0], pad)
    o_b = _mla(q, k, v, pad, tm)
    h = _peer_layer(h, [o_a, o_b], [w_out_a, w_out_b], ffn_norm[0], peer_w_query[0], peer_sub_keys[0],
                    peer_u[0], peer_v[0], tm, tl, ec)

    gw = gla_w_in[0]
    w_in_g = jnp.concatenate([gw, jnp.zeros((D, LANES - GLA_RANK), gw.dtype)], axis=1).astype(BF16)
    w_gate_p = jnp.concatenate([gla_w_gate[0], jnp.zeros((LANES - GLA_RANK, gla_w_gate.shape[-1]), F32)],
                               axis=0).astype(BF16)
    qg, kg, vg, rg, la = _gla_in(h, gla_norm[0][None], w_in_g, w_gate_p, gla_b_gate[0][None], pad, tm)
    o_g = _gla(qg, kg, vg, rg, la, gla_out_norm[0][None], tm)
    h = _peer_layer(h, [o_g], [gla_w_out[0].astype(BF16)], ffn_norm[1], peer_w_query[1], peer_sub_keys[1],
                    peer_u[1], peer_v[1], tm, tl, ec)

    skip = pad + N_META
    tf = SWA_BLOCK
    if skip % tf == 0:
        return _final_norm(h, final_norm[None], skip // tf, tf)
    return _final_norm(h, final_norm[None], 0, tf)[:, skip:]
```
